```python
import math
import jax, jax.numpy as jnp
from jax import lax
import numpy as np

D_MODEL = 1024
BATCH = 8
SEQ = 2048
DEPTH = 1
DEC_BATCH = 128
DEC_SEQ = 4
PAST_LEN = 16384
PAGE_SIZE = 128

D_MIX = D_MODEL
W_A = D_MIX // 2
W_B = D_MIX - W_A
H_A = 8
HD_A = W_A // H_A
CONV_W = 4
LRU_C = 8.0
S5_GROUP = 16
G_B = W_B // S5_GROUP
N_S5 = 64
D_FF = 4 * D_MODEL
ALPHA = (2.0 * DEPTH) ** 0.25
BETA = (8.0 * DEPTH) ** -0.25
LN_EPS = 1e-5

kernel_name = "hymba_rglru_s5_deepnorm_adaln_step"


def layer_norm(x, g, b):
    xf = x.astype(jnp.float32)
    mu = jnp.mean(xf, axis=-1, keepdims=True)
    var = jnp.mean(jnp.square(xf - mu), axis=-1, keepdims=True)
    return (xf - mu) * lax.rsqrt(var + LN_EPS) * g + b


def causal_dwconv(x, buf, w, b):
    xx = jnp.concatenate([buf.astype(x.dtype), x], axis=1)
    y = lax.conv_general_dilated(xx, w[:, None, :].astype(xx.dtype), window_strides=(1,),
                                 padding='VALID', dimension_numbers=('NWC', 'WIO', 'NWC'),
                                 feature_group_count=x.shape[-1])
    return y + b, xx[:, -(CONV_W - 1):]


def linear_scan(a, b, h0):
    b = b.at[:, 0].add(a[:, 0] * h0)
    def comb(l, r):
        return (l[0] * r[0], r[0] * l[1] + r[1])
    _, h = lax.associative_scan(comb, (a, b), axis=1)
    return h


def complex_scan(ar, ai, br, bi, h0r, h0i):
    br = br.at[:, 0].add(ar[:, 0] * h0r - ai[:, 0] * h0i)
    bi = bi.at[:, 0].add(ar[:, 0] * h0i + ai[:, 0] * h0r)
    def comb(l, r):
        a1r, a1i, b1r, b1i = l
        a2r, a2i, b2r, b2i = r
        return (a2r * a1r - a2i * a1i, a2r * a1i + a2i * a1r,
                a2r * b1r - a2i * b1i + b2r, a2r * b1i + a2i * b1r + b2i)
    _, _, hr, hi = lax.associative_scan(comb, (ar, ai, br, bi), axis=1)
    return hr, hi


def rglru(x, h0, w_r, b_r, w_i, b_i, lam, reset_first):
    B, T, _ = x.shape
    xf = x.astype(jnp.float32)
    xh = xf.reshape(B, T, H_A, HD_A)
    r = jax.nn.sigmoid(jnp.einsum('bthi,hij->bthj', xh, w_r) + b_r).reshape(B, T, W_A)
    gi = jax.nn.sigmoid(jnp.einsum('bthi,hij->bthj', xh, w_i) + b_i).reshape(B, T, W_A)
    log_a = -LRU_C * r * jax.nn.softplus(-lam.astype(jnp.float32))
    a = jnp.exp(log_a)
    mult = jnp.sqrt(-jnp.expm1(2.0 * log_a))
    if reset_first:
        mult = mult.at[:, 0].set(1.0)
    return linear_scan(a, mult * gi * xf, h0.astype(jnp.float32))


def s5_ssm(u, s0r, s0i, a_re, a_im, log_dt, b_re, b_im, c_re, c_im, d_skip):
    B, T, _ = u.shape
    uf = u.astype(jnp.float32).reshape(B, T, G_B, S5_GROUP)
    dt = jnp.exp(log_dt.astype(jnp.float32))[:, None]
    ar = a_re.astype(jnp.float32)
    ai = a_im.astype(jnp.float32)
    mag = jnp.exp(dt * ar)
    abr = mag * jnp.cos(dt * ai)
    abi = mag * jnp.sin(dt * ai)
    den = ar * ar + ai * ai
    nr = abr - 1.0
    qr = (nr * ar + abi * ai) / den
    qi = (abi * ar - nr * ai) / den
    bbr = qr[..., None] * b_re - qi[..., None] * b_im
    bbi = qr[..., None] * b_im + qi[..., None] * b_re
    bur = jnp.einsum('btgk,gnk->btgn', uf, bbr)
    bui = jnp.einsum('btgk,gnk->btgn', uf, bbi)
    full = bur.shape
    hr, hi = complex_scan(jnp.broadcast_to(abr, full), jnp.broadcast_to(abi, full),
                          bur, bui, s0r.astype(jnp.float32), s0i.astype(jnp.float32))
    y = (jnp.einsum('btgn,gkn->btgk', hr, c_re) - jnp.einsum('btgn,gkn->btgk', hi, c_im)
         + d_skip * uf)
    return y.reshape(B, T, W_B), hr[:, -1], hi[:, -1]


def trunk_layer(x, c, conv_buf, h0, s0r, s0i, reset_first, p):
    mod = jnp.dot(jax.nn.silu(c), p['ada_w']) + p['ada_b']
    sh1, sc1, g1, sh2, sc2, g2 = [m[:, None, :] for m in jnp.split(mod, 6, axis=-1)]
    h = x * (1.0 + sc1) + sh1
    z = jnp.dot(h, p['in_proj'])
    gate_a = z[..., :W_A]
    xa = z[..., W_A:2 * W_A]
    xb = z[..., 2 * W_A:]
    xa_c, conv_new = causal_dwconv(xa, conv_buf, p['conv_w'], p['conv_b'])
    hseq = rglru(xa_c, h0, p['rg_wr'], p['rg_br'], p['rg_wi'], p['rg_bi'], p['rg_lam'], reset_first)
    y_a = jax.nn.gelu(gate_a) * hseq
    y_s, sr, si = s5_ssm(xb, s0r, s0i, p['s5_a_re'], p['s5_a_im'], p['s5_log_dt'],
                         p['s5_b_re'], p['s5_b_im'], p['s5_c_re'], p['s5_c_im'], p['s5_d'])
    gy = jax.nn.gelu(y_s)
    y_b = gy * jax.nn.sigmoid(jnp.dot(gy, p['glu_w']) + p['glu_b'])
    mix = jnp.dot(jnp.concatenate([y_a, y_b], axis=-1), p['out_proj'])
    x = layer_norm(ALPHA * x + g1 * mix, p['ln1_g'], p['ln1_b'])
    h = x * (1.0 + sc2) + sh2
    f = jnp.dot(jnp.square(jax.nn.relu(jnp.dot(h, p['mlp_w1']) + p['mlp_b1'])), p['mlp_w2']) + p['mlp_b2']
    x = layer_norm(ALPHA * x + g2 * f, p['ln2_g'], p['ln2_b'])
    return x, conv_new, hseq[:, -1], sr, si


def setup_inputs(seed: int = 0) -> dict:
    key = jax.random.key(seed)
    ks = iter(jax.random.split(key, 48))
    nrm = lambda shape, s: s * jax.random.normal(next(ks), shape, jnp.float32)
    L = DEPTH
    u = jax.random.uniform(next(ks), (L, W_A), jnp.float32, minval=0.9, maxval=0.999)
    base = u ** (1.0 / LRU_C)
    rg_lam = jnp.log(base) - jnp.log1p(-base)
    log_dt = jax.random.uniform(next(ks), (L, G_B), jnp.float32,
                                minval=math.log(0.001), maxval=math.log(0.1))
    a_im = math.pi * jnp.arange(N_S5, dtype=jnp.float32)
    return {
        "x_prompt": nrm((BATCH, SEQ, D_MODEL), 1.0),
        "x_sample": nrm((DEC_BATCH, DEC_SEQ, D_MODEL), 1.0),
        "state_conv": nrm((L, DEC_BATCH, CONV_W - 1, W_A), 1.0),
        "state_rglru_h": nrm((L, DEC_BATCH, W_A), 0.5),
        "state_s5_re": nrm((L, DEC_BATCH, G_B, N_S5), 0.3),
        "state_s5_im": nrm((L, DEC_BATCH, G_B, N_S5), 0.3),
        "c_prompt": nrm((BATCH, D_MODEL), 1.0),
        "c_sample": nrm((DEC_BATCH, D_MODEL), 1.0),
        "ada_w": nrm((L, D_MODEL, 6 * D_MODEL), 0.5 * D_MODEL ** -0.5),
        "ada_b": nrm((L, 6 * D_MODEL), 0.02),
        "in_proj": nrm((L, D_MODEL, 2 * W_A + W_B), D_MODEL ** -0.5),
        "conv_w": nrm((L, CONV_W, W_A), CONV_W ** -0.5),
        "conv_b": nrm((L, W_A), 0.02),
        "rg_wr": nrm((L, H_A, HD_A, HD_A), HD_A ** -0.5),
        "rg_br": nrm((L, H_A, HD_A), 0.02),
        "rg_wi": nrm((L, H_A, HD_A, HD_A), HD_A ** -0.5),
        "rg_bi": nrm((L, H_A, HD_A), 0.02),
        "rg_lam": rg_lam,
        "s5_a_re": -0.5 + nrm((L, G_B, N_S5), 0.01),
        "s5_a_im": a_im + nrm((L, G_B, N_S5), 0.01),
        "s5_log_dt": log_dt,
        "s5_b_re": nrm((L, G_B, N_S5, S5_GROUP), (2.0 * S5_GROUP) ** -0.5),
        "s5_b_im": nrm((L, G_B, N_S5, S5_GROUP), (2.0 * S5_GROUP) ** -0.5),
        "s5_c_re": nrm((L, G_B, S5_GROUP, N_S5), (2.0 * N_S5) ** -0.5),
        "s5_c_im": nrm((L, G_B, S5_GROUP, N_S5), (2.0 * N_S5) ** -0.5),
        "s5_d": nrm((L, G_B, S5_GROUP), 1.0),
        "glu_w": nrm((L, W_B, W_B), W_B ** -0.5),
        "glu_b": nrm((L, W_B), 0.02),
        "out_proj": nrm((L, D_MIX, D_MODEL), BETA * D_MIX ** -0.5),
        "ln1_g": 1.0 + nrm((L, D_MODEL), 0.02),
        "ln1_b": nrm((L, D_MODEL), 0.02),
        "mlp_w1": nrm((L, D_MODEL, D_FF), D_MODEL ** -0.5),
        "mlp_b1": nrm((L, D_FF), 0.02),
        "mlp_w2": nrm((L, D_FF, D_MODEL), BETA * D_FF ** -0.5),
        "mlp_b2": nrm((L, D_MODEL), 0.02),
        "ln2_g": 1.0 + nrm((L, D_MODEL), 0.02),
        "ln2_b": nrm((L, D_MODEL), 0.02),
    }


def reference(x_prompt, x_sample, state_conv, state_rglru_h, state_s5_re, state_s5_im,
              c_prompt, c_sample, ada_w, ada_b, in_proj, conv_w, conv_b, rg_wr, rg_br, rg_wi,
              rg_bi, rg_lam, s5_a_re, s5_a_im, s5_log_dt, s5_b_re, s5_b_im, s5_c_re, s5_c_im,
              s5_d, glu_w, glu_b, out_proj, ln1_g, ln1_b, mlp_w1, mlp_b1, mlp_w2, mlp_b2,
              ln2_g, ln2_b):
    xp = x_prompt
    xs = x_sample
    conv_p, h_p, sre_p, sim_p = [], [], [], []
    conv_s, h_s, sre_s, sim_s = [], [], [], []
    for l in range(DEPTH):
        p = dict(ada_w=ada_w[l], ada_b=ada_b[l], in_proj=in_proj[l], conv_w=conv_w[l],
                 conv_b=conv_b[l], rg_wr=rg_wr[l], rg_br=rg_br[l], rg_wi=rg_wi[l], rg_bi=rg_bi[l],
                 rg_lam=rg_lam[l], s5_a_re=s5_a_re[l], s5_a_im=s5_a_im[l], s5_log_dt=s5_log_dt[l],
                 s5_b_re=s5_b_re[l], s5_b_im=s5_b_im[l], s5_c_re=s5_c_re[l], s5_c_im=s5_c_im[l],
                 s5_d=s5_d[l], glu_w=glu_w[l], glu_b=glu_b[l], out_proj=out_proj[l],
                 ln1_g=ln1_g[l], ln1_b=ln1_b[l], mlp_w1=mlp_w1[l], mlp_b1=mlp_b1[l],
                 mlp_w2=mlp_w2[l], mlp_b2=mlp_b2[l], ln2_g=ln2_g[l], ln2_b=ln2_b[l])
        zc = jnp.zeros((xp.shape[0], CONV_W - 1, W_A), xp.dtype)
        zh = jnp.zeros((xp.shape[0], W_A), jnp.float32)
        zs = jnp.zeros((xp.shape[0], G_B, N_S5), jnp.float32)
        xp, cb, hl, sr, si = trunk_layer(xp, c_prompt, zc, zh, zs, zs, True, p)
        conv_p.append(cb); h_p.append(hl); sre_p.append(sr); sim_p.append(si)
        xs, cb, hl, sr, si = trunk_layer(xs, c_sample, state_conv[l], state_rglru_h[l],
                                         state_s5_re[l], state_s5_im[l], False, p)
        conv_s.append(cb); h_s.append(hl); sre_s.append(sr); sim_s.append(si)
    return (xp, xs, jnp.stack(conv_p), jnp.stack(h_p), jnp.stack(sre_p), jnp.stack(sim_p),
            jnp.stack(conv_s), jnp.stack(h_s), jnp.stack(sre_s), jnp.stack(sim_s))
```

```python
import functools

import jax
import jax.numpy as jnp
from jax import lax
from jax.experimental import pallas as pl
from jax.experimental.pallas import tpu as pltpu

D_MODEL = 1024
W_A = 512
W_B = 512
H_A = 8
HD_A = W_A // H_A
CONV_W = 4
LRU_C = 8.0
S5_GROUP = 16
G_B = W_B // S5_GROUP
N_S5 = 64
S5_W = G_B * N_S5
D_FF = 4 * D_MODEL
DEPTH = 1
ALPHA = (2.0 * DEPTH) ** 0.25
LN_EPS = 1e-5

SUBLANES = 8
MXU_DIM = 256
VMEM_LIMIT_BYTES = 56 * 1024 * 1024

PROMPT_TC = 64
S5_SCAN_BLOCK = 1024
MLP_CHUNKS = 4
MOD_BLOCK_N = 1024

F32 = jnp.float32
BF16 = jnp.bfloat16


def _const_spec(shape):
    return pl.BlockSpec(shape, lambda i: (0,) * len(shape), pipeline_mode=pl.Buffered(1))


def _layer_norm(v, g, b):
    mu = jnp.mean(v, axis=-1, keepdims=True)
    c = v - mu
    var = jnp.mean(c * c, axis=-1, keepdims=True)
    return c * lax.rsqrt(var + LN_EPS) * g + b


def _mod_kernel(cp_ref, cs_ref, w_ref, b_ref, op_ref, os_ref):
    bp = cp_ref.shape[0]
    c = jnp.concatenate([cp_ref[...], cs_ref[...]], axis=0)
    s = (c * jax.nn.sigmoid(c)).astype(BF16)
    m = jnp.dot(s, w_ref[...].astype(BF16), preferred_element_type=F32) + b_ref[...]
    op_ref[...] = m[:bp]
    os_ref[...] = m[bp:]


def _mod_call(c_p, c_s, ada_w, ada_b):
    bp, bs = c_p.shape[0], c_s.shape[0]
    n = ada_w.shape[1]
    return pl.pallas_call(
        _mod_kernel,
        grid=(n // MOD_BLOCK_N,),
        in_specs=[
            _const_spec((bp, D_MODEL)),
            _const_spec((bs, D_MODEL)),
            pl.BlockSpec((D_MODEL, MOD_BLOCK_N), lambda j: (0, j)),
            pl.BlockSpec((1, MOD_BLOCK_N), lambda j: (0, j)),
        ],
        out_specs=[
            pl.BlockSpec((bp, MOD_BLOCK_N), lambda j: (0, j)),
            pl.BlockSpec((bs, MOD_BLOCK_N), lambda j: (0, j)),
        ],
        out_shape=[jax.ShapeDtypeStruct((bp, n), F32), jax.ShapeDtypeStruct((bs, n), F32)],
        compiler_params=pltpu.CompilerParams(dimension_semantics=("arbitrary",)),
        name="adaln_mod",
    )(c_p, c_s, ada_w, ada_b)


def _s5_prep_kernel(ar_ref, ai_ref, ldt_ref, br_ref, bi_ref, abr_ref, abi_ref, bbr_ref, bbi_ref):
    ar = ar_ref[...]
    ai = ai_ref[...]
    dt = jnp.exp(ldt_ref[...])
    mag = jnp.exp(dt * ar)
    abr = mag * jnp.cos(dt * ai)
    abi = mag * jnp.sin(dt * ai)
    den = ar * ar + ai * ai
    nr = abr - 1.0
    qr = (nr * ar + abi * ai) / den
    qi = (abi * ar - nr * ai) / den
    b_re = br_ref[...]
    b_im = bi_ref[...]
    abr_ref[...] = abr
    abi_ref[...] = abi
    bbr_ref[...] = qr * b_re - qi * b_im
    bbi_ref[...] = qr * b_im + qi * b_re


def _s5_prep(a_re, a_im, log_dt, b_re_t, b_im_t):
    g1n = jax.ShapeDtypeStruct((G_B, 1, N_S5), F32)
    gkn = jax.ShapeDtypeStruct((G_B, S5_GROUP, N_S5), F32)
    return pl.pallas_call(
        _s5_prep_kernel,
        out_shape=[g1n, g1n, gkn, gkn],
        name="s5_discretise",
    )(a_re, a_im, log_dt, b_re_t, b_im_t)


def _mixer_kernel(tc, bb, zero_init, reset_first, *refs):
    m = tc * bb
    n_in = 21 if zero_init else 25
    (x_ref, mod_ref, win_ref, cw_ref, cb_ref, wg_ref, gbr_ref, gbi_ref, lam_ref, abr_ref, abi_ref,
     bre_ref, bim_ref, cre_ref, cim_ref, dsk_ref, glw_ref, glb_ref, wout_ref, lng_ref,
     lnb_ref) = refs[:21]
    init_refs = refs[21:n_in]
    x1_ref, conv_ref, hl_ref, sr_ref, si_ref = refs[n_in:n_in + 5]
    (xt_s, hb_s, ga_s, u_s, xpad_s, xc_s, a_s, b_s, bur_s, bui_s, yc_s) = refs[n_in + 5:]

    pid = pl.program_id(0)
    carry_rows = (CONV_W - 1) * bb

    @pl.when(pid == 0)
    def _init_state():
        if zero_init:
            xpad_s[0:carry_rows, :] = jnp.zeros((carry_rows, W_A), F32)
            hl_ref[...] = jnp.zeros_like(hl_ref)
            sr_ref[...] = jnp.zeros_like(sr_ref)
            si_ref[...] = jnp.zeros_like(si_ref)
        else:
            c0_ref, h0_ref, s0r_ref, s0i_ref = init_refs
            xpad_s[0:carry_rows, :] = c0_ref[...]
            hl_ref[...] = h0_ref[...]
            sr_ref[...] = s0r_ref[...]
            si_ref[...] = s0i_ref[...]

    sh1 = mod_ref[:, 0:D_MODEL]
    sc1 = mod_ref[:, D_MODEL:2 * D_MODEL]
    xt = jnp.swapaxes(x_ref[...], 0, 1)
    xt_s[...] = xt.reshape(m, D_MODEL)
    hb_s[...] = (xt * (1.0 + sc1)[None] + sh1[None]).reshape(m, D_MODEL).astype(BF16)
    hb = hb_s[...]
    ga_s[...] = jnp.dot(hb, win_ref[:, 0:W_A], preferred_element_type=F32)
    xpad_s[carry_rows:carry_rows + m, :] = jnp.dot(hb, win_ref[:, W_A:2 * W_A],
                                                   preferred_element_type=F32)
    u_s[...] = jnp.dot(hb, win_ref[:, 2 * W_A:], preferred_element_type=F32)

    xc = cb_ref[...] + cw_ref[0:1, :] * xpad_s[0:m, :]
    for j in range(1, CONV_W):
        xc = xc + cw_ref[j:j + 1, :] * xpad_s[j * bb:j * bb + m, :]
    xc_s[...] = xc
    new_carry = xpad_s[tc * bb:tc * bb + carry_rows, :]
    xpad_s[0:carry_rows, :] = new_carry
    conv_ref[...] = new_carry

    clam = -LRU_C * jax.nn.softplus(-lam_ref[...])
    xcb = xc_s[...].astype(BF16)
    for hf in range(W_A // MXU_DIM):
        cs = slice(hf * MXU_DIM, (hf + 1) * MXU_DIM)
        gz = jnp.dot(xcb[:, cs], wg_ref[hf], preferred_element_type=F32)
        r = jax.nn.sigmoid(gz[:, :MXU_DIM] + gbr_ref[:, cs])
        gi = jax.nn.sigmoid(gz[:, MXU_DIM:] + gbi_ref[:, cs])
        log_a = clam[:, cs] * r
        th = jnp.tanh(log_a)
        mult = jnp.sqrt(-2.0 * th / (1.0 - th))
        if reset_first:
            first = jnp.where(pid == 0, 1.0, mult[0:bb])
            mult = jnp.concatenate([first, mult[bb:]], axis=0)
        a_s[:, cs] = jnp.exp(log_a)
        b_s[:, cs] = mult * gi * xc_s[:, cs]

    ub = u_s[...].astype(BF16)
    half_w = S5_W // 2
    for kh in range(W_B // MXU_DIM):
        lhs = ub[:, kh * MXU_DIM:(kh + 1) * MXU_DIM]
        bur_s[:, kh * half_w:(kh + 1) * half_w] = jnp.dot(lhs, bre_ref[kh],
                                                          preferred_element_type=F32)
        bui_s[:, kh * half_w:(kh + 1) * half_w] = jnp.dot(lhs, bim_ref[kh],
                                                          preferred_element_type=F32)

    n_row_tiles = bb // SUBLANES
    unroll = min(tc, 8)

    def rg_rows(rb, carry):
        r0 = pl.multiple_of(rb * SUBLANES, SUBLANES)

        def step(t, h):
            idx = pl.multiple_of(t * bb + r0, SUBLANES)
            h = a_s[pl.ds(idx, SUBLANES), :] * h + b_s[pl.ds(idx, SUBLANES), :]
            b_s[pl.ds(idx, SUBLANES), :] = h
            return h

        hl_ref[pl.ds(r0, SUBLANES), :] = lax.fori_loop(
            0, tc, step, hl_ref[pl.ds(r0, SUBLANES), :], unroll=unroll)
        return carry

    lax.fori_loop(0, n_row_tiles, rg_rows, 0)

    for cbk in range(S5_W // S5_SCAN_BLOCK):
        cs = slice(cbk * S5_SCAN_BLOCK, (cbk + 1) * S5_SCAN_BLOCK)
        ar = jnp.broadcast_to(abr_ref[:, cs], (SUBLANES, S5_SCAN_BLOCK))
        ai = jnp.broadcast_to(abi_ref[:, cs], (SUBLANES, S5_SCAN_BLOCK))

        def s5_rows(rb, carry, cs=cs, ar=ar, ai=ai):
            r0 = pl.multiple_of(rb * SUBLANES, SUBLANES)

            def step(t, h):
                hr, hi = h
                idx = pl.multiple_of(t * bb + r0, SUBLANES)
                nhr = ar * hr - ai * hi + bur_s[pl.ds(idx, SUBLANES), cs]
                nhi = ar * hi + ai * hr + bui_s[pl.ds(idx, SUBLANES), cs]
                bur_s[pl.ds(idx, SUBLANES), cs] = nhr
                bui_s[pl.ds(idx, SUBLANES), cs] = nhi
                return nhr, nhi

            hr, hi = lax.fori_loop(
                0, tc, step, (sr_ref[pl.ds(r0, SUBLANES), cs], si_ref[pl.ds(r0, SUBLANES), cs]),
                unroll=unroll)
            sr_ref[pl.ds(r0, SUBLANES), cs] = hr
            si_ref[pl.ds(r0, SUBLANES), cs] = hi
            return carry

        lax.fori_loop(0, n_row_tiles, s5_rows, 0)

    yc_s[:, 0:W_A] = (jax.nn.gelu(ga_s[...]) * b_s[...]).astype(BF16)
    hrb = bur_s[...].astype(BF16)
    hib = bui_s[...].astype(BF16)
    ys_parts = []
    for nh in range(W_B // MXU_DIM):
        ks = slice(nh * half_w, (nh + 1) * half_w)
        cs = slice(nh * MXU_DIM, (nh + 1) * MXU_DIM)
        ys = (jnp.dot(hrb[:, ks], cre_ref[nh], preferred_element_type=F32)
              - jnp.dot(hib[:, ks], cim_ref[nh], preferred_element_type=F32)
              + dsk_ref[:, cs] * u_s[:, cs])
        ys_parts.append(ys)
    gy = jax.nn.gelu(jnp.concatenate(ys_parts, axis=1))
    gl = jnp.dot(gy.astype(BF16), glw_ref[...], preferred_element_type=F32) + glb_ref[...]
    yc_s[:, W_A:] = (gy * jax.nn.sigmoid(gl)).astype(BF16)
    mix = jnp.dot(yc_s[...], wout_ref[...], preferred_element_type=F32)
    g1 = mod_ref[:, 2 * D_MODEL:3 * D_MODEL]
    v = ALPHA * xt_s[...].reshape(tc, bb, D_MODEL) + g1[None] * mix.reshape(tc, bb, D_MODEL)
    x1_ref[...] = _layer_norm(v, lng_ref[...].reshape(1, 1, D_MODEL),
                              lnb_ref[...].reshape(1, 1, D_MODEL))


def _mixer_call(x, mod, weights, init_state, *, tc, reset_first):
    bb, t_total, _ = x.shape
    m = tc * bb
    carry_rows = (CONV_W - 1) * bb
    zero_init = init_state is None
    const_inputs = [mod] + list(weights) + ([] if zero_init else list(init_state))
    in_specs = [pl.BlockSpec((bb, tc, D_MODEL), lambda i: (0, i, 0))]
    in_specs += [_const_spec(a.shape) for a in const_inputs]
    out_shape = [
        jax.ShapeDtypeStruct((t_total, bb, D_MODEL), F32),
        jax.ShapeDtypeStruct((carry_rows, W_A), F32),
        jax.ShapeDtypeStruct((bb, W_A), F32),
        jax.ShapeDtypeStruct((bb, S5_W), F32),
        jax.ShapeDtypeStruct((bb, S5_W), F32),
    ]
    out_specs = [pl.BlockSpec((tc, bb, D_MODEL), lambda i: (i, 0, 0))]
    out_specs += [pl.BlockSpec(s.shape, lambda i: (0, 0)) for s in out_shape[1:]]
    scratch = [
        pltpu.VMEM((m, D_MODEL), F32),
        pltpu.VMEM((m, D_MODEL), BF16),
        pltpu.VMEM((m, W_A), F32),
        pltpu.VMEM((m, W_B), F32),
        pltpu.VMEM((m + carry_rows, W_A), F32),
        pltpu.VMEM((m, W_A), F32),
        pltpu.VMEM((m, W_A), F32),
        pltpu.VMEM((m, W_A), F32),
        pltpu.VMEM((m, S5_W), F32),
        pltpu.VMEM((m, S5_W), F32),
        pltpu.VMEM((m, W_A + W_B), BF16),
    ]
    return pl.pallas_call(
        functools.partial(_mixer_kernel, tc, bb, zero_init, reset_first),
        grid=(t_total // tc,),
        in_specs=in_specs,
        out_specs=out_specs,
        out_shape=out_shape,
        scratch_shapes=scratch,
        compiler_params=pltpu.CompilerParams(dimension_semantics=("arbitrary",),
                                             vmem_limit_bytes=VMEM_LIMIT_BYTES),
        name="mixer_sublayer",
    )(x, *const_inputs)


def _mlp_kernel(tc, bb, x_ref, mod_ref, w1_ref, b1_ref, w2_ref, b2_ref, lng_ref, lnb_ref, o_ref,
                h_s, f_s):
    m = tc * bb
    sh2 = mod_ref[:, 3 * D_MODEL:4 * D_MODEL]
    sc2 = mod_ref[:, 4 * D_MODEL:5 * D_MODEL]
    g2 = mod_ref[:, 5 * D_MODEL:6 * D_MODEL]
    h_s[...] = (x_ref[...] * (1.0 + sc2)[None] + sh2[None]).reshape(m, D_MODEL).astype(BF16)
    cw = D_FF // MLP_CHUNKS
    for c in range(MLP_CHUNKS):
        cs = slice(c * cw, (c + 1) * cw)
        a = jnp.dot(h_s[...], w1_ref[:, cs], preferred_element_type=F32) + b1_ref[:, cs]
        a = jnp.maximum(a, 0.0)
        p = jnp.dot((a * a).astype(BF16), w2_ref[cs, :], preferred_element_type=F32)
        if c == 0:
            f_s[...] = p
        else:
            f_s[...] += p
    f = (f_s[...] + b2_ref[...]).reshape(tc, bb, D_MODEL)
    v = ALPHA * x_ref[...] + g2[None] * f
    y = _layer_norm(v, lng_ref[...].reshape(1, 1, D_MODEL), lnb_ref[...].reshape(1, 1, D_MODEL))
    o_ref[...] = jnp.swapaxes(y, 0, 1)


def _mlp_call(x_tb, mod, w1, b1, w2, b2, lng, lnb, *, tc):
    t_total, bb, _ = x_tb.shape
    m = tc * bb
    consts = [mod, w1, b1, w2, b2, lng, lnb]
    return pl.pallas_call(
        functools.partial(_mlp_kernel, tc, bb),
        grid=(t_total // tc,),
        in_specs=[pl.BlockSpec((tc, bb, D_MODEL), lambda i: (i, 0, 0))]
        + [_const_spec(a.shape) for a in consts],
        out_specs=pl.BlockSpec((bb, tc, D_MODEL), lambda i: (0, i, 0)),
        out_shape=jax.ShapeDtypeStruct((bb, t_total, D_MODEL), F32),
        scratch_shapes=[pltpu.VMEM((m, D_MODEL), BF16), pltpu.VMEM((m, D_MODEL), F32)],
        compiler_params=pltpu.CompilerParams(dimension_semantics=("arbitrary",),
                                             vmem_limit_bytes=VMEM_LIMIT_BYTES),
        name="mlp_sublayer",
    )(x_tb, *consts)


def _block_diag(blocks):
    n, r, c = blocks.shape
    eye = jnp.eye(n, dtype=jnp.bool_)
    full = jnp.where(eye[:, None, :, None], blocks[:, :, None, :], jnp.zeros((), blocks.dtype))
    return full.reshape(n * r, n * c)


def _gate_weights(w_r, w_i):
    heads_per_tile = MXU_DIM // HD_A
    halves = []
    for hf in range(H_A // heads_per_tile):
        hs = slice(hf * heads_per_tile, (hf + 1) * heads_per_tile)
        halves.append(jnp.concatenate([_block_diag(w_r[hs]), _block_diag(w_i[hs])], axis=1))
    return jnp.stack(halves).astype(BF16)


def _s5_in_weights(bb_gkn):
    groups_per_half = MXU_DIM // S5_GROUP
    halves = [_block_diag(bb_gkn[h * groups_per_half:(h + 1) * groups_per_half])
              for h in range(G_B // groups_per_half)]
    return jnp.stack(halves).astype(BF16)


def _s5_out_weights(c_gkn):
    groups_per_half = MXU_DIM // S5_GROUP
    c_gnk = jnp.swapaxes(c_gkn, 1, 2)
    halves = [_block_diag(c_gnk[h * groups_per_half:(h + 1) * groups_per_half])
              for h in range(G_B // groups_per_half)]
    return jnp.stack(halves).astype(BF16)


def kernel(x_prompt, x_sample, state_conv, state_rglru_h, state_s5_re, state_s5_im, c_prompt, c_sample, ada_w, ada_b, in_proj, conv_w, conv_b, rg_wr, rg_br, rg_wi, rg_bi, rg_lam, s5_a_re, s5_a_im, s5_log_dt, s5_b_re, s5_b_im, s5_c_re, s5_c_im, s5_d, glu_w, glu_b, out_proj, ln1_g, ln1_b, mlp_w1, mlp_b1, mlp_w2, mlp_b2, ln2_g, ln2_b):
    assert ada_w.shape[0] == DEPTH == 1
    l = 0
    bp = x_prompt.shape[0]
    bs, ts = x_sample.shape[0], x_sample.shape[1]

    mod_p, mod_s = _mod_call(c_prompt, c_sample, ada_w[l], ada_b[l].reshape(1, -1))

    abr, abi, bbr, bbi = _s5_prep(
        s5_a_re[l].reshape(G_B, 1, N_S5), s5_a_im[l].reshape(G_B, 1, N_S5),
        s5_log_dt[l].reshape(G_B, 1, 1),
        jnp.swapaxes(s5_b_re[l], 1, 2), jnp.swapaxes(s5_b_im[l], 1, 2))

    row = lambda v: v.reshape(1, -1)
    mixer_weights = [
        in_proj[l].astype(BF16), conv_w[l], row(conv_b[l]),
        _gate_weights(rg_wr[l], rg_wi[l]), row(rg_br[l]), row(rg_bi[l]), row(rg_lam[l]),
        row(abr), row(abi),
        _s5_in_weights(bbr), _s5_in_weights(bbi),
        _s5_out_weights(s5_c_re[l]), _s5_out_weights(s5_c_im[l]),
        row(s5_d[l]), glu_w[l].astype(BF16), row(glu_b[l]), out_proj[l].astype(BF16),
        row(ln1_g[l]), row(ln1_b[l]),
    ]
    mlp_weights = [mlp_w1[l].astype(BF16), row(mlp_b1[l]), mlp_w2[l].astype(BF16), row(mlp_b2[l]),
                   row(ln2_g[l]), row(ln2_b[l])]

    x1_p, conv_p, h_p, sre_p, sim_p = _mixer_call(
        x_prompt, mod_p, mixer_weights, None, tc=PROMPT_TC, reset_first=True)
    y_p = _mlp_call(x1_p, mod_p, *mlp_weights, tc=PROMPT_TC)

    conv0 = jnp.swapaxes(state_conv[l], 0, 1).reshape((CONV_W - 1) * bs, W_A)
    init_s = [conv0, state_rglru_h[l], state_s5_re[l].reshape(bs, S5_W),
              state_s5_im[l].reshape(bs, S5_W)]
    x1_s, conv_s, h_s, sre_s, sim_s = _mixer_call(
        x_sample, mod_s, mixer_weights, init_s, tc=ts, reset_first=False)
    y_s = _mlp_call(x1_s, mod_s, *mlp_weights, tc=ts)

    def conv_out(c, b):
        return jnp.swapaxes(c.reshape(CONV_W - 1, b, W_A), 0, 1)[None]

    def s5_out(s, b):
        return s.reshape(1, b, G_B, N_S5)

    return (y_p, y_s,
            conv_out(conv_p, bp), h_p[None], s5_out(sre_p, bp), s5_out(sim_p, bp),
            conv_out(conv_s, bs), h_s[None], s5_out(sre_s, bs), s5_out(sim_s, bs))
```
